```python
import math
import jax, jax.numpy as jnp
from jax import lax
import numpy as np


D_MODEL = 1024
BATCH = 2
SEQ = 8192
DEPTH = 4
DEC_BATCH = 16
DEC_SEQ = 2048
PAST_LEN = 128

GRID_W = 64
HD = 64
H_A = 4
Q_LORA = 256
KV_LORA = 128
NOPE_A = 64
ROPE_A = 32
V_A = 64
H_B = 4
DIL_PATTERNS = ((128, 1), (512, 4), (2048, 16))
N_BUCKETS = 32
MAX_DISTANCE = 1024
H_C = 8
KV_C = 2
D_MIX = H_A * V_A + H_B * HD + H_C * HD
IN_SIZES = (Q_LORA, KV_LORA, ROPE_A, H_B * HD, H_B * HD, H_B * HD, H_C * HD, KV_C * HD, KV_C * HD)
N_IN = sum(IN_SIZES)
D_FF = 2816
CONV_W = 3
Q_BLOCK = 128
ROPE_THETA = 10000.0
EPS = 1e-6

kernel_name = 'hybrid_parallel_encoder'


def rms_norm(x, g):
    xf = x.astype(jnp.float32)
    y = xf * lax.rsqrt(jnp.mean(xf * xf, axis=-1, keepdims=True) + EPS)
    return (y * g.astype(jnp.float32)).astype(x.dtype)


def rope_angles(pos, n_freq):
    inv = ROPE_THETA ** (-jnp.arange(n_freq, dtype=jnp.float32) / n_freq)
    return pos.astype(jnp.float32)[:, None] * inv[None, :]


def apply_rope(x, ang):
    shape = (ang.shape[0],) + (1,) * (x.ndim - 3) + (ang.shape[1],)
    cos = jnp.cos(ang).reshape(shape).astype(x.dtype)
    sin = jnp.sin(ang).reshape(shape).astype(x.dtype)
    x1, x2 = jnp.split(x, 2, axis=-1)
    return jnp.concatenate([x1 * cos - x2 * sin, x2 * cos + x1 * sin], axis=-1)


def t5_bucket(rel):
    half = N_BUCKETS // 2
    exact = half // 2
    n = np.abs(rel)
    large = exact + (np.log(np.maximum(n, 1) / exact) / math.log(MAX_DISTANCE / exact) * (half - exact)).astype(np.int32)
    large = np.minimum(large, half - 1)
    return (np.where(rel > 0, half, 0) + np.where(n < exact, n, large)).astype(np.int32)


def block_attention(q, k, v, scale):
    b, s = q.shape[0], q.shape[1]
    nq = s // Q_BLOCK
    qb = q.reshape((b, nq, Q_BLOCK) + q.shape[2:]).swapaxes(0, 1)

    def one(qblk):
        logits = jnp.einsum('bqhgd,bkhd->bhgqk', qblk, k).astype(jnp.float32) * scale
        p = jax.nn.softmax(logits, axis=-1)
        return jnp.einsum('bhgqk,bkhd->bqhgd', p.astype(v.dtype), v)

    out = lax.map(one, qb)
    return out.swapaxes(0, 1).reshape(b, s, -1)


def dilated_window_attention(q, k, v, rel_bias, window, dilation):
    b, s, h, dh = q.shape
    w = window // (2 * dilation)
    L = s // dilation
    nb = -(-L // w)
    lp = nb * w

    def strided(a):
        return a.reshape(b, L, dilation, h, dh).transpose(0, 2, 1, 3, 4)

    qs = jnp.pad(strided(q), ((0, 0), (0, 0), (0, lp - L), (0, 0), (0, 0))).reshape(b, dilation, nb, w, h, dh)

    def windows(a):
        ap = jnp.pad(strided(a), ((0, 0), (0, 0), (w, lp - L + w), (0, 0), (0, 0))).reshape(b, dilation, nb + 2, w, h, dh)
        return jnp.concatenate([ap[:, :, :-2], ap[:, :, 1:-1], ap[:, :, 2:]], axis=3)

    kw, vw = windows(k), windows(v)
    rel = np.arange(3 * w)[None, :] - w - np.arange(w)[:, None]
    key_l = np.arange(nb)[:, None] * w + np.arange(3 * w)[None, :] - w
    mask = (np.abs(rel) <= w)[None] & ((key_l >= 0) & (key_l < L))[:, None, :]
    bias = jnp.transpose(rel_bias[t5_bucket(rel * dilation)], (2, 0, 1)).astype(jnp.float32)
    logits = jnp.einsum('brnqhd,brnkhd->brnhqk', qs, kw).astype(jnp.float32) * (dh ** -0.5) + bias
    logits = jnp.where(mask[:, None], logits, -jnp.inf)
    m = jnp.max(logits, axis=-1, keepdims=True)
    e = jnp.exp(logits - m)
    den = jnp.sum(e, axis=-1, keepdims=True)
    o = jnp.einsum('brnhqk,brnkhd->brnqhd', (e / den).astype(v.dtype), vw)
    lse = (m + jnp.log(den))[..., 0]
    o = o.reshape(b, dilation, lp, h, dh)[:, :, :L].transpose(0, 2, 1, 3, 4).reshape(b, s, h, dh)
    lse = lse.transpose(0, 1, 2, 4, 3).reshape(b, dilation, lp, h)[:, :, :L].transpose(0, 2, 1, 3).reshape(b, s, h)
    return o, lse


def mla_mixer(cq, ckv, krope, q_lora_norm, w_uq, kv_norm, w_ukv, ang):
    b, s, _ = cq.shape
    q = (rms_norm(cq, q_lora_norm) @ w_uq).reshape(b, s, H_A, NOPE_A + ROPE_A)
    kv = (rms_norm(ckv, kv_norm) @ w_ukv).reshape(b, s, H_A, NOPE_A + V_A)
    q = jnp.concatenate([q[..., :NOPE_A], apply_rope(q[..., NOPE_A:], ang)], axis=-1)
    kr = apply_rope(krope, ang)
    k = jnp.concatenate([kv[..., :NOPE_A], jnp.broadcast_to(kr[:, :, None, :], (b, s, H_A, ROPE_A))], axis=-1)
    v = kv[..., NOPE_A:]
    return block_attention(q[:, :, :, None, :], k, v, (NOPE_A + ROPE_A) ** -0.5)


def dilated_mixer(zq, zk, zv, rel_bias):
    b, s, _ = zq.shape
    q = zq.reshape(b, s, H_B, HD)
    k = zk.reshape(b, s, H_B, HD)
    v = zv.reshape(b, s, H_B, HD)
    outs, lses = [], []
    for window, dilation in DIL_PATTERNS:
        o, l = dilated_window_attention(q, k, v, rel_bias, window, dilation)
        outs.append(o)
        lses.append(l)
    wts = jax.nn.softmax(jnp.stack(lses, axis=0), axis=0)
    out = jnp.einsum('pbsh,pbshd->bshd', wts, jnp.stack(outs, axis=0).astype(jnp.float32))
    return out.astype(zq.dtype).reshape(b, s, H_B * HD)


def axial_rope(x, ang_row, ang_col):
    half = HD // 2
    return jnp.concatenate([apply_rope(x[..., :half], ang_row), apply_rope(x[..., half:], ang_col)], axis=-1)


def gqa_axial_mixer(zq, zk, zv, c_q_norm, c_k_norm, ang_row, ang_col):
    b, s, _ = zq.shape
    q = axial_rope(rms_norm(zq.reshape(b, s, H_C, HD), c_q_norm), ang_row, ang_col)
    k = axial_rope(rms_norm(zk.reshape(b, s, KV_C, HD), c_k_norm), ang_row, ang_col)
    v = zv.reshape(b, s, KV_C, HD)
    return block_attention(q.reshape(b, s, KV_C, H_C // KV_C, HD), k, v, HD ** -0.5)


def conv_gated_mlp(h, w_up, conv_w, conv_b, w_down):
    s = h.shape[1]
    u = h @ w_up
    half = CONV_W // 2
    up = jnp.pad(u, ((0, 0), (half, half), (0, 0)))
    c = conv_b
    for i in range(CONV_W):
        c = c + up[:, i:i + s] * conv_w[i]
    gate, val = jnp.split(c, 2, axis=-1)
    return (jax.nn.gelu(gate, approximate=True) * val) @ w_down


def trunk(x, rel_bias, norm_mix_pre, norm_mix_post, w_in, q_lora_norm, w_uq, kv_norm, w_ukv,
          c_q_norm, c_k_norm, w_out, norm_ffn_pre, norm_ffn_post, w_up, conv_w, conv_b, w_down):
    b, s, _ = x.shape
    rows = s // GRID_W
    ang_a = rope_angles(jnp.arange(s), ROPE_A // 2)
    row_ids = jnp.repeat(jnp.arange(rows), GRID_W)
    col_ids = jnp.tile(jnp.arange(GRID_W), rows)
    ang_row = rope_angles(row_ids, HD // 4)
    ang_col = rope_angles(col_ids, HD // 4)
    splits = [int(c) for c in np.cumsum(IN_SIZES)[:-1]]
    for l in range(DEPTH):
        h = rms_norm(x, norm_mix_pre[l])
        z = h @ w_in[l]
        a_q, a_kv, a_kr, b_q, b_k, b_v, c_q, c_k, c_v = jnp.split(z, splits, axis=-1)
        oa = mla_mixer(a_q, a_kv, a_kr, q_lora_norm[l], w_uq[l], kv_norm[l], w_ukv[l], ang_a)
        ob = dilated_mixer(b_q, b_k, b_v, rel_bias)
        oc = gqa_axial_mixer(c_q, c_k, c_v, c_q_norm[l], c_k_norm[l], ang_row, ang_col)
        mix = jnp.concatenate([oa, ob, oc], axis=-1) @ w_out[l]
        x = x + rms_norm(mix, norm_mix_post[l])
        h = rms_norm(x, norm_ffn_pre[l])
        x = x + rms_norm(conv_gated_mlp(h, w_up[l], conv_w[l], conv_b[l], w_down[l]), norm_ffn_post[l])
    return x


def setup_inputs(seed: int = 0) -> dict:
    key = jax.random.key(seed)
    ks = jax.random.split(key, 20)
    f32 = jnp.float32

    def nrm(k, shape, scale):
        return jax.random.normal(k, shape, f32) * scale

    def gain(k, shape):
        return 1.0 + 0.05 * jax.random.normal(k, shape, f32)

    return {
        'x_prompt': nrm(ks[0], (BATCH, SEQ, D_MODEL), 1.0),
        'x_sample': nrm(ks[1], (DEC_BATCH, DEC_SEQ, D_MODEL), 1.0),
        'rel_bias': nrm(ks[2], (N_BUCKETS, H_B), 0.5),
        'norm_mix_pre': gain(ks[3], (DEPTH, D_MODEL)),
        'norm_mix_post': gain(ks[4], (DEPTH, D_MODEL)),
        'w_in': nrm(ks[5], (DEPTH, D_MODEL, N_IN), D_MODEL ** -0.5),
        'q_lora_norm': gain(ks[6], (DEPTH, Q_LORA)),
        'w_uq': nrm(ks[7], (DEPTH, Q_LORA, H_A * (NOPE_A + ROPE_A)), Q_LORA ** -0.5),
        'kv_norm': gain(ks[8], (DEPTH, KV_LORA)),
        'w_ukv': nrm(ks[9], (DEPTH, KV_LORA, H_A * (NOPE_A + V_A)), KV_LORA ** -0.5),
        'c_q_norm': gain(ks[10], (DEPTH, HD)),
        'c_k_norm': gain(ks[11], (DEPTH, HD)),
        'w_out': nrm(ks[12], (DEPTH, D_MIX, D_MODEL), D_MIX ** -0.5),
        'norm_ffn_pre': gain(ks[13], (DEPTH, D_MODEL)),
        'norm_ffn_post': gain(ks[14], (DEPTH, D_MODEL)),
        'w_up': nrm(ks[15], (DEPTH, D_MODEL, 2 * D_FF), D_MODEL ** -0.5),
        'conv_w': nrm(ks[16], (DEPTH, CONV_W, 2 * D_FF), CONV_W ** -0.5),
        'conv_b': nrm(ks[17], (DEPTH, 2 * D_FF), 0.01),
        'w_down': nrm(ks[18], (DEPTH, D_FF, D_MODEL), D_FF ** -0.5),
    }


def reference(x_prompt, x_sample, rel_bias, norm_mix_pre, norm_mix_post, w_in, q_lora_norm, w_uq, kv_norm,
              w_ukv, c_q_norm, c_k_norm, w_out, norm_ffn_pre, norm_ffn_post, w_up, conv_w, conv_b, w_down):
    y_prompt = trunk(x_prompt, rel_bias, norm_mix_pre, norm_mix_post, w_in, q_lora_norm, w_uq, kv_norm, w_ukv,
                     c_q_norm, c_k_norm, w_out, norm_ffn_pre, norm_ffn_post, w_up, conv_w, conv_b, w_down)
    y_sample = trunk(x_sample, rel_bias, norm_mix_pre, norm_mix_post, w_in, q_lora_norm, w_uq, kv_norm, w_ukv,
                     c_q_norm, c_k_norm, w_out, norm_ffn_pre, norm_ffn_post, w_up, conv_w, conv_b, w_down)
    return (y_prompt, y_sample)
```

```python
import functools
import math

import numpy as np
import jax
import jax.numpy as jnp
from jax import lax
from jax.experimental import pallas as pl
from jax.experimental.pallas import tpu as pltpu

D_MODEL = 1024
GRID_W = 64
HD = 64
H_A, Q_LORA, KV_LORA, NOPE_A, ROPE_A, V_A = 4, 256, 128, 64, 32, 64
H_B = 4
DIL_PATTERNS = ((128, 1), (512, 4), (2048, 16))
N_BUCKETS, MAX_DISTANCE = 32, 1024
H_C, KV_C = 8, 2
D_FF = 2816
ROPE_THETA = 10000.0
EPS = 1e-6
BAND = 64
assert all(w // (2 * d) == BAND for w, d in DIL_PATTERNS)

LANES = 128
HALF = LANES // 2
V7X_VMEM_LIMIT = 56 * 1024 * 1024

TM_PROJ = 512
TQ_ATT, TK_ATT = 512, 512
TL_DIL = 128
TF_FFN = 256
NEG = -1e30

N_INP = 2048
SEG_AQ, SEG_AKV, SEG_AKR, SEG_BQ, SEG_BK, SEG_BV, SEG_CQ, SEG_CK, SEG_CV = (
    0, 256, 384, 512, 768, 1024, 1280, 1792, 1920)
C_HEAD_ORDER = (0, 4, 1, 5, 2, 6, 3, 7)

_F32 = jnp.float32
_BF16 = jnp.bfloat16
_NT = (((1,), (1,)), ((), ()))


def _rms(x, g):
    return x * lax.rsqrt(jnp.mean(x * x, axis=-1, keepdims=True) + EPS) * g


def _dot(a, b):
    return jnp.dot(a, b, preferred_element_type=_F32)


def _lane_iota(shape):
    return lax.broadcasted_iota(jnp.int32, shape, len(shape) - 1)


def _inproj_kernel(x_ref, g_ref, win_ref, gq_ref, wuq_ref, gkv_ref, wukv_ref, gcq_ref, gck_ref,
                   ca_ref, sa_ref, cc_ref, sc_ref,
                   qa_ref, ka_ref, va_ref, bq_ref, bk_ref, bv_ref, qc_ref, kc_ref, vc_ref):
    tm = x_ref.shape[0]
    h = _rms(x_ref[...], g_ref[...]).astype(_BF16)
    z = _dot(h, win_ref[...])

    lane = _lane_iota((tm, LANES))
    low_half = lane < HALF

    q = _dot(_rms(z[:, SEG_AQ:SEG_AKV], gq_ref[...]).astype(_BF16), wuq_ref[...])
    kv = _dot(_rms(z[:, SEG_AKV:SEG_AKR], gkv_ref[...]).astype(_BF16), wukv_ref[...])
    ca, sa = ca_ref[...], sa_ref[...]
    first_rot_half = (lane >= NOPE_A) & (lane < NOPE_A + ROPE_A // 2)

    def rope_a(t):
        partner = jnp.where(first_rot_half, pltpu.roll(t, LANES - ROPE_A // 2, 1), pltpu.roll(t, ROPE_A // 2, 1))
        return t * ca + partner * sa

    kr = rope_a(z[:, SEG_AKR:SEG_BQ])
    scale_a = (NOPE_A + ROPE_A) ** -0.5
    for hh in range(H_A):
        sl = slice(hh * LANES, (hh + 1) * LANES)
        qa_ref[0, hh] = (rope_a(q[:, sl]) * scale_a).astype(_BF16)
        ka_ref[0, hh] = (kv[:, sl] + kr).astype(_BF16)
    for p in range(H_A // 2):
        va_ref[0, p] = kv[:, (H_A + p) * LANES:(H_A + p + 1) * LANES].astype(_BF16)

    bq_ref[...] = (z[:, SEG_BQ:SEG_BK] * HD ** -0.5).astype(_BF16)
    bk_ref[...] = z[:, SEG_BK:SEG_BV].astype(_BF16)
    bv_ref[...] = z[:, SEG_BV:SEG_CQ].astype(_BF16)

    cc, sc = cc_ref[...], sc_ref[...]
    rot_first = (lane & (HD // 4)) == 0

    def norm_rope_pair(t, g):
        sq = t * t
        ss_lo = jnp.sum(jnp.where(low_half, sq, 0.0), axis=-1, keepdims=True)
        ss_hi = jnp.sum(jnp.where(low_half, 0.0, sq), axis=-1, keepdims=True)
        inv = jnp.where(low_half, lax.rsqrt(ss_lo * (1.0 / HD) + EPS), lax.rsqrt(ss_hi * (1.0 / HD) + EPS))
        tn = t * inv * g
        partner = jnp.where(rot_first, pltpu.roll(tn, LANES - HD // 4, 1), pltpu.roll(tn, HD // 4, 1))
        return tn * cc + partner * sc

    for p in range(H_C // 2):
        t = z[:, SEG_CQ + p * LANES:SEG_CQ + (p + 1) * LANES]
        qc_ref[0, p] = (norm_rope_pair(t, gcq_ref[...]) * HD ** -0.5).astype(_BF16)
    kc_ref[...] = norm_rope_pair(z[:, SEG_CK:SEG_CV], gck_ref[...]).astype(_BF16)
    vc_ref[...] = z[:, SEG_CV:N_INP].astype(_BF16)


def _inproj(xf, bsz, seq, lw, tabs):
    n = xf.shape[0]
    tm = TM_PROJ
    nt = seq // tm
    tok = lambda t: (t, 0)
    pos = lambda t: (t % nt, 0)
    heads = lambda t: (t // nt, 0, t % nt, 0)
    const = lambda t: (0, 0)
    full = lambda a: pl.BlockSpec(a.shape, const)
    in_specs = [
        pl.BlockSpec((tm, D_MODEL), tok), full(lw['g_mix_pre']), full(lw['w_in']), full(lw['g_q']), full(lw['w_uq']),
        full(lw['g_kv']), full(lw['w_ukv']), full(lw['g_cq']), full(lw['g_ck']),
        pl.BlockSpec((tm, LANES), pos), pl.BlockSpec((tm, LANES), pos),
        pl.BlockSpec((tm, LANES), pos), pl.BlockSpec((tm, LANES), pos),
    ]
    out_shape = [
        jax.ShapeDtypeStruct((bsz, H_A, seq, LANES), _BF16),
        jax.ShapeDtypeStruct((bsz, H_A, seq, LANES), _BF16),
        jax.ShapeDtypeStruct((bsz, H_A // 2, seq, LANES), _BF16),
        jax.ShapeDtypeStruct((n, H_B * HD), _BF16),
        jax.ShapeDtypeStruct((n, H_B * HD), _BF16),
        jax.ShapeDtypeStruct((n, H_B * HD), _BF16),
        jax.ShapeDtypeStruct((bsz, H_C // 2, seq, LANES), _BF16),
        jax.ShapeDtypeStruct((n, KV_C * HD), _BF16),
        jax.ShapeDtypeStruct((n, KV_C * HD), _BF16),
    ]
    out_specs = [
        pl.BlockSpec((1, H_A, tm, LANES), heads), pl.BlockSpec((1, H_A, tm, LANES), heads),
        pl.BlockSpec((1, H_A // 2, tm, LANES), heads),
        pl.BlockSpec((tm, H_B * HD), tok), pl.BlockSpec((tm, H_B * HD), tok), pl.BlockSpec((tm, H_B * HD), tok),
        pl.BlockSpec((1, H_C // 2, tm, LANES), heads),
        pl.BlockSpec((tm, KV_C * HD), tok), pl.BlockSpec((tm, KV_C * HD), tok),
    ]
    return pl.pallas_call(
        _inproj_kernel, grid=(n // tm,), in_specs=in_specs, out_specs=out_specs, out_shape=out_shape,
        compiler_params=pltpu.CompilerParams(dimension_semantics=("arbitrary",), vmem_limit_bytes=V7X_VMEM_LIMIT),
        name="inproj",
    )(xf, lw['g_mix_pre'], lw['w_in'], lw['g_q'], lw['w_uq'], lw['g_kv'], lw['w_ukv'], lw['g_cq'], lw['g_ck'],
      tabs['ca'], tabs['sa'], tabs['cc'], tabs['sc'])


def _flash_kernel(q_ref, k_ref, v_ref, o_ref, *, shared_qk, tk):
    tq = o_ref.shape[1]
    nk = v_ref.shape[2] // tk
    lane = _lane_iota((tq, LANES))
    low_half = lane < HALF
    outs = []
    for e in range(2):
        if shared_qk:
            qp = q_ref[0, 0]
            q = jnp.where(low_half if e == 0 else jnp.logical_not(low_half), qp, jnp.zeros_like(qp))
            kidx = 0
        else:
            q = q_ref[0, e]
            kidx = e

        def body(j, carry, q=q, kidx=kidx):
            m, l, acc = carry
            start = pl.multiple_of(j * tk, tk)
            k = k_ref[0, kidx, pl.ds(start, tk), :]
            v = v_ref[0, 0, pl.ds(start, tk), :]
            s = lax.dot_general(q, k, _NT, preferred_element_type=_F32)
            m_new = jnp.maximum(m, jnp.max(s, axis=-1, keepdims=True))
            alpha = jnp.exp(m - m_new)
            p = jnp.exp(s - m_new)
            l = alpha * l + jnp.sum(p, axis=-1, keepdims=True)
            acc = alpha * acc + _dot(p.astype(_BF16), v)
            return m_new, l, acc

        init = (jnp.full((tq, 1), -jnp.inf, _F32), jnp.zeros((tq, 1), _F32), jnp.zeros((tq, LANES), _F32))
        _, l, acc = lax.fori_loop(0, nk, body, init)
        outs.append(acc / l)
    o_ref[0] = jnp.where(low_half, outs[0], outs[1]).astype(o_ref.dtype)


def _flash(q, k, v, bsz, seq, n_pairs, shared_qk):
    tq, tk = TQ_ATT, TK_ATT
    if shared_qk:
        q_spec = pl.BlockSpec((1, 1, tq, LANES), lambda b, p, i: (b, p, i, 0))
        k_spec = pl.BlockSpec((1, 1, seq, LANES), lambda b, p, i: (b, 0, 0, 0))
        v_spec = pl.BlockSpec((1, 1, seq, LANES), lambda b, p, i: (b, 0, 0, 0))
    else:
        q_spec = pl.BlockSpec((1, 2, tq, LANES), lambda b, p, i: (b, p, i, 0))
        k_spec = pl.BlockSpec((1, 2, seq, LANES), lambda b, p, i: (b, p, 0, 0))
        v_spec = pl.BlockSpec((1, 1, seq, LANES), lambda b, p, i: (b, p, 0, 0))
    return pl.pallas_call(
        functools.partial(_flash_kernel, shared_qk=shared_qk, tk=tk),
        grid=(bsz, n_pairs, seq // tq),
        in_specs=[q_spec, k_spec, v_spec],
        out_specs=pl.BlockSpec((1, tq, LANES), lambda b, p, i: (b, i, p)),
        out_shape=jax.ShapeDtypeStruct((bsz, seq, n_pairs * LANES), _BF16),
        compiler_params=pltpu.CompilerParams(dimension_semantics=("arbitrary",) * 3, vmem_limit_bytes=V7X_VMEM_LIMIT),
        name="flash_c" if shared_qk else "flash_a",
    )(q, k, v)


def _dilated_kernel(q_ref, kp_ref, kc_ref, kn_ref, vp_ref, vc_ref, vn_ref, bias_ref, o_ref, lse_ref):
    tl = q_ref.shape[1]
    nkeys = tl + 2 * BAND
    j, nt = pl.program_id(2), pl.num_programs(2)
    q = q_ref[0]
    k = jnp.concatenate([kp_ref[0], kc_ref[0], kn_ref[0]], axis=0)
    v = jnp.concatenate([vp_ref[0], vc_ref[0], vn_ref[0]], axis=0)
    col = _lane_iota((tl, nkeys))
    in_seq = ((col >= BAND) | (j > 0)) & ((col < tl + BAND) | (j < nt - 1))
    lane = _lane_iota((tl, LANES))
    low_half = lane < HALF
    for p in range(H_B // 2):
        sl = slice(p * LANES, (p + 1) * LANES)
        qp, kp, vp = q[:, sl], k[:, sl], v[:, sl]
        o_e, lse_e = [], []
        for e in range(2):
            qm = jnp.where(low_half if e == 0 else jnp.logical_not(low_half), qp, jnp.zeros_like(qp))
            s = lax.dot_general(qm, kp, _NT, preferred_element_type=_F32) + bias_ref[2 * p + e]
            s = jnp.where(in_seq, s, NEG)
            m = jnp.max(s, axis=-1, keepdims=True)
            ex = jnp.exp(s - m)
            den = jnp.sum(ex, axis=-1, keepdims=True)
            o_e.append(_dot(ex.astype(_BF16), vp) / den)
            lse_e.append(m + jnp.log(den))
        o_ref[0, :, sl] = jnp.where(low_half, o_e[0], o_e[1])
        lse_ref[0, :, sl] = jnp.where(low_half, lse_e[0], lse_e[1])


def _dilated(bq, bk, bv, bias, bsz, seq, dil):
    n = bq.shape[0]
    length = seq // dil
    tl = TL_DIL
    nbt = tl // BAND
    nt = length // tl
    width = H_B * HD
    view = lambda a: a.reshape(bsz, length, dil * width)
    cur = lambda b, r, j: (b, j, r)
    prev = lambda b, r, j: (b, jnp.maximum(j * nbt - 1, 0), r)
    nxt = lambda b, r, j: (b, jnp.minimum((j + 1) * nbt, length // BAND - 1), r)
    tile = pl.BlockSpec((1, tl, width), cur)
    halo_p = pl.BlockSpec((1, BAND, width), prev)
    halo_n = pl.BlockSpec((1, BAND, width), nxt)
    o, lse = pl.pallas_call(
        _dilated_kernel, grid=(bsz, dil, nt),
        in_specs=[tile, halo_p, tile, halo_n, halo_p, tile, halo_n,
                  pl.BlockSpec(bias.shape, lambda b, r, j: (0, 0, 0))],
        out_specs=[tile, tile],
        out_shape=[jax.ShapeDtypeStruct((bsz, length, dil * width), _F32)] * 2,
        compiler_params=pltpu.CompilerParams(dimension_semantics=("arbitrary",) * 3),
        name=f"dilated_d{dil}",
    )(view(bq), view(bk), view(bk), view(bk), view(bv), view(bv), view(bv), bias)
    return o.reshape(n, width), lse.reshape(n, width)


def _outproj_kernel(x_ref, oa_ref, o1_ref, l1_ref, o2_ref, l2_ref, o3_ref, l3_ref, oc_ref, w_ref, g_ref, y_ref):
    l1, l2, l3 = l1_ref[...], l2_ref[...], l3_ref[...]
    m = jnp.maximum(jnp.maximum(l1, l2), l3)
    e1, e2, e3 = jnp.exp(l1 - m), jnp.exp(l2 - m), jnp.exp(l3 - m)
    ob = (e1 * o1_ref[...] + e2 * o2_ref[...] + e3 * o3_ref[...]) / (e1 + e2 + e3)
    mix = jnp.concatenate([oa_ref[...], ob.astype(_BF16), oc_ref[...]], axis=-1)
    y_ref[...] = x_ref[...] + _rms(_dot(mix, w_ref[...]), g_ref[...])


def _outproj(xf, oa, parts, oc, lw):
    n = xf.shape[0]
    tm = TM_PROJ
    tok = lambda t: (t, 0)
    const = lambda t: (0, 0)
    bw = H_B * HD
    args = [xf, oa]
    specs = [pl.BlockSpec((tm, D_MODEL), tok), pl.BlockSpec((tm, H_A * V_A), tok)]
    for o, lse in parts:
        args += [o, lse]
        specs += [pl.BlockSpec((tm, bw), tok), pl.BlockSpec((tm, bw), tok)]
    args += [oc, lw['w_out'], lw['g_mix_post']]
    specs += [pl.BlockSpec((tm, H_C * HD), tok), pl.BlockSpec(lw['w_out'].shape, const),
              pl.BlockSpec(lw['g_mix_post'].shape, const)]
    return pl.pallas_call(
        _outproj_kernel, grid=(n // tm,), in_specs=specs,
        out_specs=pl.BlockSpec((tm, D_MODEL), tok),
        out_shape=jax.ShapeDtypeStruct((n, D_MODEL), _F32),
        compiler_params=pltpu.CompilerParams(dimension_semantics=("arbitrary",), vmem_limit_bytes=V7X_VMEM_LIMIT),
        name="outproj",
    )(*args)


def _ffn_kernel(x_ref, xp_ref, xn_ref, gpre_ref, wug_ref, wuv_ref, cw_ref, wd_ref, gpost_ref, y_ref,
                h_scr, hh_scr, acc_scr, *, tiles_per_seq):
    tm = x_ref.shape[0]
    i = pl.program_id(0)
    x = x_ref[...]
    g = gpre_ref[...]
    h_scr[...] = _rms(x, g).astype(_BF16)
    halo = jnp.concatenate([xp_ref[...], xn_ref[...]], axis=0)
    hh_scr[...] = _rms(halo, g).astype(_BF16)
    acc_scr[...] = jnp.zeros_like(acc_scr)
    has_prev = jnp.where(i % tiles_per_seq != 0, 1.0, 0.0).astype(_F32)
    has_next = jnp.where((i + 1) % tiles_per_seq != 0, 1.0, 0.0).astype(_F32)
    row = lax.broadcasted_iota(jnp.int32, (tm, TF_FFN), 0)
    first_row, last_row = row == 0, row == tm - 1
    halo_rows = xp_ref.shape[0]

    def conv(u, uh, w):
        prev_row = uh[halo_rows - 1:halo_rows] * has_prev
        next_row = uh[halo_rows:halo_rows + 1] * has_next
        up = jnp.where(first_row, prev_row, pltpu.roll(u, 1, 0))
        un = jnp.where(last_row, next_row, pltpu.roll(u, tm - 1, 0))
        return w[3:4] + up * w[0:1] + u * w[1:2] + un * w[2:3]

    def chunk(c, carry):
        hb, hh = h_scr[...], hh_scr[...]
        wg, wv, cw = wug_ref[c], wuv_ref[c], cw_ref[c]
        gate = conv(_dot(hb, wg), _dot(hh, wg), cw[0:4])
        val = conv(_dot(hb, wv), _dot(hh, wv), cw[4:8])
        act = (jax.nn.gelu(gate, approximate=True) * val).astype(_BF16)
        acc_scr[...] += _dot(act, wd_ref[c])
        return carry

    lax.fori_loop(0, wug_ref.shape[0], chunk, 0)
    y_ref[...] = x + _rms(acc_scr[...], gpost_ref[...])


def _ffn(xf, seq, lw):
    n = xf.shape[0]
    tm = TM_PROJ
    halo = 8
    tok = lambda t: (t, 0)
    resident = lambda a: pl.BlockSpec(a.shape, lambda t: (0,) * a.ndim, pipeline_mode=pl.Buffered(1))
    prev = lambda t: (jnp.maximum(t * (tm // halo) - 1, 0), 0)
    nxt = lambda t: (jnp.minimum((t + 1) * (tm // halo), n // halo - 1), 0)
    return pl.pallas_call(
        functools.partial(_ffn_kernel, tiles_per_seq=seq // tm), grid=(n // tm,),
        in_specs=[pl.BlockSpec((tm, D_MODEL), tok), pl.BlockSpec((halo, D_MODEL), prev),
                  pl.BlockSpec((halo, D_MODEL), nxt), resident(lw['g_ffn_pre']), resident(lw['w_up_g']),
                  resident(lw['w_up_v']), resident(lw['conv']), resident(lw['w_down']), resident(lw['g_ffn_post'])],
        out_specs=pl.BlockSpec((tm, D_MODEL), tok),
        out_shape=jax.ShapeDtypeStruct((n, D_MODEL), _F32),
        scratch_shapes=[pltpu.VMEM((tm, D_MODEL), _BF16), pltpu.VMEM((2 * halo, D_MODEL), _BF16),
                        pltpu.VMEM((tm, D_MODEL), _F32)],
        compiler_params=pltpu.CompilerParams(dimension_semantics=("arbitrary",), vmem_limit_bytes=V7X_VMEM_LIMIT),
        name="ffn",
    )(xf, xf, xf, lw['g_ffn_pre'], lw['w_up_g'], lw['w_up_v'], lw['conv'], lw['w_down'], lw['g_ffn_post'])


def _t5_bucket(rel):
    half = N_BUCKETS // 2
    exact = half // 2
    n = np.abs(rel)
    large = exact + (np.log(np.maximum(n, 1) / exact) / math.log(MAX_DISTANCE / exact) * (half - exact)).astype(np.int32)
    large = np.minimum(large, half - 1)
    return (np.where(rel > 0, half, 0) + np.where(n < exact, n, large)).astype(np.int32)


def _bias_tiles(rel_bias):
    tl = TL_DIL
    rel = np.arange(tl + 2 * BAND)[None, :] - BAND - np.arange(tl)[:, None]
    in_band = jnp.asarray(np.abs(rel) <= BAND)[None]
    tiles = []
    for _, dil in DIL_PATTERNS:
        b = jnp.transpose(rel_bias[_t5_bucket(rel * dil)], (2, 0, 1)).astype(_F32)
        tiles.append(jnp.where(in_band, b, NEG))
    return tiles


def _rope_tables(seq):
    pos = jnp.arange(seq)

    def angles(p, n_freq):
        inv = ROPE_THETA ** (-jnp.arange(n_freq, dtype=_F32) / n_freq)
        return p.astype(_F32)[:, None] * inv[None, :]

    ang_a = angles(pos, ROPE_A // 2)
    ones = jnp.ones((seq, NOPE_A), _F32)
    zpad = jnp.zeros((seq, LANES - NOPE_A - ROPE_A), _F32)
    znope = jnp.zeros((seq, NOPE_A), _F32)
    ca = jnp.concatenate([ones, jnp.cos(ang_a), jnp.cos(ang_a), zpad], axis=-1)
    sa = jnp.concatenate([znope, -jnp.sin(ang_a), jnp.sin(ang_a), zpad], axis=-1)
    ang_r = angles(pos // GRID_W, HD // 4)
    ang_c = angles(pos % GRID_W, HD // 4)
    cc = jnp.concatenate([jnp.cos(ang_r), jnp.cos(ang_r), jnp.cos(ang_c), jnp.cos(ang_c)] * 2, axis=-1)
    sc = jnp.concatenate([-jnp.sin(ang_r), jnp.sin(ang_r), -jnp.sin(ang_c), jnp.sin(ang_c)] * 2, axis=-1)
    return dict(ca=ca, sa=sa, cc=cc, sc=sc)


def _prep_layer_weights(norm_mix_pre, norm_mix_post, w_in, q_lora_norm, w_uq, kv_norm, w_ukv, c_q_norm, c_k_norm,
                        w_out, norm_ffn_pre, norm_ffn_post, w_up, conv_w, conv_b, w_down):
    depth = w_in.shape[0]
    off = np.cumsum((0, Q_LORA, KV_LORA, ROPE_A, H_B * HD, H_B * HD, H_B * HD, H_C * HD, KV_C * HD, KV_C * HD))
    seg = [w_in[..., off[i]:off[i + 1]] for i in range(9)]
    zc = lambda n: jnp.zeros((depth, D_MODEL, n), w_in.dtype)
    kr_seg = jnp.concatenate([zc(NOPE_A), seg[2], zc(LANES - NOPE_A - ROPE_A)], axis=-1)
    order = np.asarray(C_HEAD_ORDER)
    cq = seg[6].reshape(depth, D_MODEL, H_C, HD)[:, :, order].reshape(depth, D_MODEL, H_C * HD)
    w_in_p = jnp.concatenate([seg[0], seg[1], kr_seg, seg[3], seg[4], seg[5], cq, seg[7], seg[8]], axis=-1)
    assert w_in_p.shape[-1] == N_INP

    uq = w_uq.reshape(depth, Q_LORA, H_A, NOPE_A + ROPE_A)
    uq = jnp.pad(uq, ((0, 0), (0, 0), (0, 0), (0, LANES - NOPE_A - ROPE_A))).reshape(depth, Q_LORA, H_A * LANES)
    ukv = w_ukv.reshape(depth, KV_LORA, H_A, NOPE_A + V_A)
    uk = jnp.pad(ukv[..., :NOPE_A], ((0, 0), (0, 0), (0, 0), (0, LANES - NOPE_A))).reshape(depth, KV_LORA, H_A * LANES)
    uv = ukv[..., NOPE_A:].reshape(depth, KV_LORA, H_A * V_A)
    w_ukv_p = jnp.concatenate([uk, uv], axis=-1)

    n_a, n_b = H_A * V_A, H_B * HD
    oc_rows = w_out[:, n_a + n_b:].reshape(depth, H_C, HD, D_MODEL)[:, order].reshape(depth, H_C * HD, D_MODEL)
    w_out_p = jnp.concatenate([w_out[:, :n_a + n_b], oc_rows], axis=1)

    nc = D_FF // TF_FFN
    chunked_cols = lambda a: a.reshape(depth, a.shape[1], nc, TF_FFN).transpose(0, 2, 1, 3)
    conv = jnp.concatenate([conv_w, conv_b[:, None, :]], axis=1)
    conv = jnp.concatenate([chunked_cols(conv[..., :D_FF]), chunked_cols(conv[..., D_FF:])], axis=2)

    row = lambda a: a[:, None, :].astype(_F32)
    tile2 = lambda a: jnp.concatenate([a, a], axis=-1)
    return dict(
        g_mix_pre=row(norm_mix_pre), g_mix_post=row(norm_mix_post), w_in=w_in_p.astype(_BF16),
        g_q=row(q_lora_norm), w_uq=uq.astype(_BF16), g_kv=row(kv_norm), w_ukv=w_ukv_p.astype(_BF16),
        g_cq=row(tile2(c_q_norm)), g_ck=row(tile2(c_k_norm)), w_out=w_out_p.astype(_BF16),
        g_ffn_pre=row(norm_ffn_pre), g_ffn_post=row(norm_ffn_post),
        w_up_g=chunked_cols(w_up[..., :D_FF]).astype(_BF16), w_up_v=chunked_cols(w_up[..., D_FF:]).astype(_BF16),
        conv=conv.astype(_F32), w_down=w_down.reshape(depth, nc, TF_FFN, D_MODEL).astype(_BF16),
    )


def _trunk(x, weights, bias_tiles):
    bsz, seq, _ = x.shape
    assert seq % TM_PROJ == 0 and seq % TQ_ATT == 0 and seq % TK_ATT == 0
    assert all((seq // d) % TL_DIL == 0 for _, d in DIL_PATTERNS)
    n = bsz * seq
    tabs = _rope_tables(seq)
    xf = x.reshape(n, D_MODEL)
    depth = weights['w_in'].shape[0]
    for layer in range(depth):
        lw = {k: v[layer] for k, v in weights.items()}
        qa, ka, va, bq, bk, bv, qc, kc, vc = _inproj(xf, bsz, seq, lw, tabs)
        oa = _flash(qa, ka, va, bsz, seq, H_A // 2, shared_qk=False)
        oc = _flash(qc, kc.reshape(bsz, 1, seq, LANES), vc.reshape(bsz, 1, seq, LANES), bsz, seq, H_C // 2,
                    shared_qk=True)
        parts = [_dilated(bq, bk, bv, bias_tiles[i], bsz, seq, dil) for i, (_, dil) in enumerate(DIL_PATTERNS)]
        xf = _outproj(xf, oa.reshape(n, H_A * V_A), parts, oc.reshape(n, H_C * HD), lw)
        xf = _ffn(xf, seq, lw)
    return xf.reshape(bsz, seq, D_MODEL)


def kernel(x_prompt, x_sample, rel_bias, norm_mix_pre, norm_mix_post, w_in, q_lora_norm, w_uq, kv_norm, w_ukv,
           c_q_norm, c_k_norm, w_out, norm_ffn_pre, norm_ffn_post, w_up, conv_w, conv_b, w_down):
    weights = _prep_layer_weights(norm_mix_pre, norm_mix_post, w_in, q_lora_norm, w_uq, kv_norm, w_ukv, c_q_norm,
                                  c_k_norm, w_out, norm_ffn_pre, norm_ffn_post, w_up, conv_w, conv_b, w_down)
    bias_tiles = _bias_tiles(rel_bias)
    return _trunk(x_prompt, weights, bias_tiles), _trunk(x_sample, weights, bias_tiles)
```

```python
import functools
import math

import numpy as np
import jax
import jax.numpy as jnp
from jax import lax
from jax.experimental import pallas as pl
from jax.experimental.pallas import tpu as pltpu

D_MODEL = 1024
GRID_W = 64
HD = 64
H_A, Q_LORA, KV_LORA, NOPE_A, ROPE_A, V_A = 4, 256, 128, 64, 32, 64
H_B = 4
DIL_PATTERNS = ((128, 1), (512, 4), (2048, 16))
N_BUCKETS, MAX_DISTANCE = 32, 1024
H_C, KV_C = 8, 2
D_FF = 2816
ROPE_THETA = 10000.0
EPS = 1e-6
BAND = 64
assert all(w // (2 * d) == BAND for w, d in DIL_PATTERNS)

LANES = 128
HALF = LANES // 2
V7X_VMEM_LIMIT = 56 * 1024 * 1024

TM_PROJ = 512
TQ_ATT, TK_ATT = 512, 512
SUM_ROWS = 16
TL_DIL = 128
TF_FFN = 256
NEG = -1e30
LOG2E = math.log2(math.e)
assert TM_PROJ == TK_ATT

N_INP = 2048
SEG_AQ, SEG_AKV, SEG_AKR, SEG_BQ, SEG_BK, SEG_BV, SEG_CQ, SEG_CK, SEG_CV = (
    0, 256, 384, 512, 768, 1024, 1280, 1792, 1920)
C_HEAD_ORDER = (0, 4, 1, 5, 2, 6, 3, 7)

_F32 = jnp.float32
_BF16 = jnp.bfloat16
_NT = (((1,), (1,)), ((), ()))


def _rms(x, g):
    return x * lax.rsqrt(jnp.mean(x * x, axis=-1, keepdims=True) + EPS) * g


def _dot(a, b):
    return jnp.dot(a, b, preferred_element_type=_F32)


def _lane_iota(shape):
    return lax.broadcasted_iota(jnp.int32, shape, len(shape) - 1)


def _inproj_kernel(x_ref, g_ref, win_ref, gq_ref, wuq_ref, gkv_ref, wukv_ref, gcq_ref, gck_ref,
                   ca_ref, sa_ref, cc_ref, sc_ref,
                   qa_ref, ka_ref, va_ref, bq_ref, bk_ref, bv_ref, qc_ref, kc_ref, vc_ref):
    tm = x_ref.shape[0]
    h = _rms(x_ref[...], g_ref[...]).astype(_BF16)
    z = _dot(h, win_ref[...])

    lane = _lane_iota((tm, LANES))
    low_half = lane < HALF

    q = _dot(_rms(z[:, SEG_AQ:SEG_AKV], gq_ref[...]).astype(_BF16), wuq_ref[...])
    kv = _dot(_rms(z[:, SEG_AKV:SEG_AKR], gkv_ref[...]).astype(_BF16), wukv_ref[...])
    ca, sa = ca_ref[...], sa_ref[...]
    first_rot_half = (lane >= NOPE_A) & (lane < NOPE_A + ROPE_A // 2)

    def rope_a(t):
        partner = jnp.where(first_rot_half, pltpu.roll(t, LANES - ROPE_A // 2, 1), pltpu.roll(t, ROPE_A // 2, 1))
        return t * ca + partner * sa

    kr = rope_a(z[:, SEG_AKR:SEG_BQ])
    scale_a = (NOPE_A + ROPE_A) ** -0.5 * LOG2E
    for hh in range(H_A):
        sl = slice(hh * LANES, (hh + 1) * LANES)
        qa_ref[0, hh] = (rope_a(q[:, sl]) * scale_a).astype(_BF16)
        ka_ref[0, hh] = (kv[:, sl] + kr).astype(_BF16)
    for p in range(H_A // 2):
        va_ref[0, p, 0] = kv[:, (H_A + p) * LANES:(H_A + p + 1) * LANES].T.astype(_BF16)

    bq_ref[...] = (z[:, SEG_BQ:SEG_BK] * HD ** -0.5).astype(_BF16)
    bk_ref[...] = z[:, SEG_BK:SEG_BV].astype(_BF16)
    bv_ref[...] = z[:, SEG_BV:SEG_CQ].astype(_BF16)

    cc, sc = cc_ref[...], sc_ref[...]
    rot_first = (lane & (HD // 4)) == 0

    def norm_rope_pair(t, g):
        sq = t * t
        ss_lo = jnp.sum(jnp.where(low_half, sq, 0.0), axis=-1, keepdims=True)
        ss_hi = jnp.sum(jnp.where(low_half, 0.0, sq), axis=-1, keepdims=True)
        inv = jnp.where(low_half, lax.rsqrt(ss_lo * (1.0 / HD) + EPS), lax.rsqrt(ss_hi * (1.0 / HD) + EPS))
        tn = t * inv * g
        partner = jnp.where(rot_first, pltpu.roll(tn, LANES - HD // 4, 1), pltpu.roll(tn, HD // 4, 1))
        return tn * cc + partner * sc

    for p in range(H_C // 2):
        t = z[:, SEG_CQ + p * LANES:SEG_CQ + (p + 1) * LANES]
        qc_ref[0, p] = (norm_rope_pair(t, gcq_ref[...]) * (HD ** -0.5 * LOG2E)).astype(_BF16)
    kc_ref[...] = norm_rope_pair(z[:, SEG_CK:SEG_CV], gck_ref[...]).astype(_BF16)
    vc_ref[0, 0, 0] = z[:, SEG_CV:N_INP].T.astype(_BF16)


def _inproj(xf, bsz, seq, lw, tabs):
    n = xf.shape[0]
    tm = TM_PROJ
    nt = seq // tm
    tok = lambda t: (t, 0)
    pos = lambda t: (t % nt, 0)
    heads = lambda t: (t // nt, 0, t % nt, 0)
    vtile = lambda t: (t // nt, 0, t % nt, 0, 0)
    const = lambda t: (0, 0)
    full = lambda a: pl.BlockSpec(a.shape, const)
    in_specs = [
        pl.BlockSpec((tm, D_MODEL), tok), full(lw['g_mix_pre']), full(lw['w_in']), full(lw['g_q']), full(lw['w_uq']),
        full(lw['g_kv']), full(lw['w_ukv']), full(lw['g_cq']), full(lw['g_ck']),
        pl.BlockSpec((tm, LANES), pos), pl.BlockSpec((tm, LANES), pos),
        pl.BlockSpec((tm, LANES), pos), pl.BlockSpec((tm, LANES), pos),
    ]
    out_shape = [
        jax.ShapeDtypeStruct((bsz, H_A, seq, LANES), _BF16),
        jax.ShapeDtypeStruct((bsz, H_A, seq, LANES), _BF16),
        jax.ShapeDtypeStruct((bsz, H_A // 2, nt, LANES, tm), _BF16),
        jax.ShapeDtypeStruct((n, H_B * HD), _BF16),
        jax.ShapeDtypeStruct((n, H_B * HD), _BF16),
        jax.ShapeDtypeStruct((n, H_B * HD), _BF16),
        jax.ShapeDtypeStruct((bsz, H_C // 2, seq, LANES), _BF16),
        jax.ShapeDtypeStruct((n, KV_C * HD), _BF16),
        jax.ShapeDtypeStruct((bsz, 1, nt, LANES, tm), _BF16),
    ]
    out_specs = [
        pl.BlockSpec((1, H_A, tm, LANES), heads), pl.BlockSpec((1, H_A, tm, LANES), heads),
        pl.BlockSpec((1, H_A // 2, 1, LANES, tm), vtile),
        pl.BlockSpec((tm, H_B * HD), tok), pl.BlockSpec((tm, H_B * HD), tok), pl.BlockSpec((tm, H_B * HD), tok),
        pl.BlockSpec((1, H_C // 2, tm, LANES), heads),
        pl.BlockSpec((tm, KV_C * HD), tok), pl.BlockSpec((1, 1, 1, LANES, tm), vtile),
    ]
    return pl.pallas_call(
        _inproj_kernel, grid=(n // tm,), in_specs=in_specs, out_specs=out_specs, out_shape=out_shape,
        compiler_params=pltpu.CompilerParams(dimension_semantics=("arbitrary",), vmem_limit_bytes=V7X_VMEM_LIMIT),
        name="inproj",
    )(xf, lw['g_mix_pre'], lw['w_in'], lw['g_q'], lw['w_uq'], lw['g_kv'], lw['w_ukv'], lw['g_cq'], lw['g_ck'],
      tabs['ca'], tabs['sa'], tabs['cc'], tabs['sc'])


def _flash_kernel(q_ref, k_ref, vt_ref, o_ref, st_scr, *, shared_qk):
    tq = o_ref.shape[1]
    nk, _, tk = vt_ref.shape[2:]
    qs = []
    for e in range(2):
        if shared_qk:
            qp = q_ref[0, 0]
            in_head = (_lane_iota(qp.shape) < HALF) == (e == 0)
            qs.append(jnp.where(in_head, qp, jnp.zeros_like(qp)))
        else:
            qs.append(q_ref[0, e])

    def scores(j, slot):
        start = pl.multiple_of(j * tk, tk)
        for e in range(2):
            st_scr[slot, e] = lax.dot_general(k_ref[0, 0 if shared_qk else e, pl.ds(start, tk), :], qs[e], _NT,
                                              preferred_element_type=_F32)

    ones_rows = jnp.ones((SUM_ROWS, tk), _BF16)

    def absorb(j, slot, state):
        vt = vt_ref[0, 0, j]
        new = []
        for e in range(2):
            m, acc = state[e]
            m_new = jnp.maximum(m, jnp.max(st_scr[slot, e], axis=0, keepdims=True))
            alpha = jnp.exp2(m - m_new)
            pt = jnp.exp2(st_scr[slot, e] - m_new).astype(_BF16)
            v_aug = jnp.concatenate([vt[e * HALF:(e + 1) * HALF], ones_rows], axis=0)
            new.append((m_new, alpha * acc + _dot(v_aug, pt)))
        return tuple(new)

    def body(i, state):
        scores(2 * i + 1, 1)
        state = absorb(2 * i, 0, state)
        scores(2 * i + 2, 0)
        return absorb(2 * i + 1, 1, state)

    init = (jnp.full((1, tq), -jnp.inf, _F32), jnp.zeros((HALF + SUM_ROWS, tq), _F32))
    scores(0, 0)
    state = lax.fori_loop(0, nk // 2 - 1, body, (init, init))
    scores(nk - 1, 1)
    state = absorb(nk - 2, 0, state)
    (_, acc0), (_, acc1) = absorb(nk - 1, 1, state)
    out_t = jnp.concatenate([a[:HALF] / a[HALF:HALF + 1] for a in (acc0, acc1)], axis=0)
    o_ref[0] = out_t.T.astype(o_ref.dtype)


def _flash(q, k, vt, bsz, seq, n_pairs, shared_qk):
    tq = TQ_ATT
    vt_block = (1, 1) + vt.shape[2:]
    if shared_qk:
        q_spec = pl.BlockSpec((1, 1, tq, LANES), lambda b, p, i: (b, p, i, 0))
        k_spec = pl.BlockSpec((1, 1, seq, LANES), lambda b, p, i: (b, 0, 0, 0))
        v_spec = pl.BlockSpec(vt_block, lambda b, p, i: (b, 0, 0, 0, 0))
    else:
        q_spec = pl.BlockSpec((1, 2, tq, LANES), lambda b, p, i: (b, p, i, 0))
        k_spec = pl.BlockSpec((1, 2, seq, LANES), lambda b, p, i: (b, p, 0, 0))
        v_spec = pl.BlockSpec(vt_block, lambda b, p, i: (b, p, 0, 0, 0))
    return pl.pallas_call(
        functools.partial(_flash_kernel, shared_qk=shared_qk),
        grid=(bsz, n_pairs, seq // tq),
        in_specs=[q_spec, k_spec, v_spec],
        out_specs=pl.BlockSpec((1, tq, LANES), lambda b, p, i: (b, i, p)),
        out_shape=jax.ShapeDtypeStruct((bsz, seq, n_pairs * LANES), _BF16),
        scratch_shapes=[pltpu.VMEM((2, 2, vt.shape[-1], tq), _F32)],
        compiler_params=pltpu.CompilerParams(dimension_semantics=("arbitrary",) * 3, vmem_limit_bytes=V7X_VMEM_LIMIT),
        name="flash_c" if shared_qk else "flash_a",
    )(q, k, vt)


def _dilated_kernel(q_ref, kp_ref, kc_ref, kn_ref, vp_ref, vc_ref, vn_ref, bias_ref, o_ref, lse_ref):
    tl = q_ref.shape[1]
    nkeys = tl + 2 * BAND
    j, nt = pl.program_id(2), pl.num_programs(2)
    q = q_ref[0]
    k = jnp.concatenate([kp_ref[0], kc_ref[0], kn_ref[0]], axis=0)
    v = jnp.concatenate([vp_ref[0], vc_ref[0], vn_ref[0]], axis=0)
    col = _lane_iota((tl, nkeys))
    in_seq = ((col >= BAND) | (j > 0)) & ((col < tl + BAND) | (j < nt - 1))
    lane = _lane_iota((tl, LANES))
    low_half = lane < HALF
    for p in range(H_B // 2):
        sl = slice(p * LANES, (p + 1) * LANES)
        qp, kp, vp = q[:, sl], k[:, sl], v[:, sl]
        o_e, lse_e = [], []
        for e in range(2):
            qm = jnp.where(low_half if e == 0 else jnp.logical_not(low_half), qp, jnp.zeros_like(qp))
            s = lax.dot_general(qm, kp, _NT, preferred_element_type=_F32) + bias_ref[2 * p + e]
            s = jnp.where(in_seq, s, NEG)
            m = jnp.max(s, axis=-1, keepdims=True)
            ex = jnp.exp(s - m)
            den = jnp.sum(ex, axis=-1, keepdims=True)
            o_e.append(_dot(ex.astype(_BF16), vp) / den)
            lse_e.append(m + jnp.log(den))
        o_ref[0, :, sl] = jnp.where(low_half, o_e[0], o_e[1])
        lse_ref[0, :, sl] = jnp.where(low_half, lse_e[0], lse_e[1])


def _dilated(bq, bk, bv, bias, bsz, seq, dil):
    n = bq.shape[0]
    length = seq // dil
    tl = TL_DIL
    nbt = tl // BAND
    nt = length // tl
    width = H_B * HD
    view = lambda a: a.reshape(bsz, length, dil * width)
    cur = lambda b, r, j: (b, j, r)
    prev = lambda b, r, j: (b, jnp.maximum(j * nbt - 1, 0), r)
    nxt = lambda b, r, j: (b, jnp.minimum((j + 1) * nbt, length // BAND - 1), r)
    tile = pl.BlockSpec((1, tl, width), cur)
    halo_p = pl.BlockSpec((1, BAND, width), prev)
    halo_n = pl.BlockSpec((1, BAND, width), nxt)
    o, lse = pl.pallas_call(
        _dilated_kernel, grid=(bsz, dil, nt),
        in_specs=[tile, halo_p, tile, halo_n, halo_p, tile, halo_n,
                  pl.BlockSpec(bias.shape, lambda b, r, j: (0, 0, 0))],
        out_specs=[tile, tile],
        out_shape=[jax.ShapeDtypeStruct((bsz, length, dil * width), _F32)] * 2,
        compiler_params=pltpu.CompilerParams(dimension_semantics=("arbitrary",) * 3),
        name=f"dilated_d{dil}",
    )(view(bq), view(bk), view(bk), view(bk), view(bv), view(bv), view(bv), bias)
    return o.reshape(n, width), lse.reshape(n, width)


def _outproj_kernel(x_ref, oa_ref, o1_ref, l1_ref, o2_ref, l2_ref, o3_ref, l3_ref, oc_ref, w_ref, g_ref, y_ref):
    l1, l2, l3 = l1_ref[...], l2_ref[...], l3_ref[...]
    m = jnp.maximum(jnp.maximum(l1, l2), l3)
    e1, e2, e3 = jnp.exp(l1 - m), jnp.exp(l2 - m), jnp.exp(l3 - m)
    ob = (e1 * o1_ref[...] + e2 * o2_ref[...] + e3 * o3_ref[...]) / (e1 + e2 + e3)
    mix = jnp.concatenate([oa_ref[...], ob.astype(_BF16), oc_ref[...]], axis=-1)
    y_ref[...] = x_ref[...] + _rms(_dot(mix, w_ref[...]), g_ref[...])


def _outproj(xf, oa, parts, oc, lw):
    n = xf.shape[0]
    tm = TM_PROJ
    tok = lambda t: (t, 0)
    const = lambda t: (0, 0)
    bw = H_B * HD
    args = [xf, oa]
    specs = [pl.BlockSpec((tm, D_MODEL), tok), pl.BlockSpec((tm, H_A * V_A), tok)]
    for o, lse in parts:
        args += [o, lse]
        specs += [pl.BlockSpec((tm, bw), tok), pl.BlockSpec((tm, bw), tok)]
    args += [oc, lw['w_out'], lw['g_mix_post']]
    specs += [pl.BlockSpec((tm, H_C * HD), tok), pl.BlockSpec(lw['w_out'].shape, const),
              pl.BlockSpec(lw['g_mix_post'].shape, const)]
    return pl.pallas_call(
        _outproj_kernel, grid=(n // tm,), in_specs=specs,
        out_specs=pl.BlockSpec((tm, D_MODEL), tok),
        out_shape=jax.ShapeDtypeStruct((n, D_MODEL), _F32),
        compiler_params=pltpu.CompilerParams(dimension_semantics=("arbitrary",), vmem_limit_bytes=V7X_VMEM_LIMIT),
        name="outproj",
    )(*args)


def _ffn_kernel(x_ref, xp_ref, xn_ref, gpre_ref, wug_ref, wuv_ref, cw_ref, wd_ref, gpost_ref, y_ref,
                h_scr, hh_scr, acc_scr, *, tiles_per_seq):
    tm = x_ref.shape[0]
    i = pl.program_id(0)
    x = x_ref[...]
    g = gpre_ref[...]
    h_scr[...] = _rms(x, g).astype(_BF16)
    halo = jnp.concatenate([xp_ref[...], xn_ref[...]], axis=0)
    hh_scr[...] = _rms(halo, g).astype(_BF16)
    acc_scr[...] = jnp.zeros_like(acc_scr)
    has_prev = jnp.where(i % tiles_per_seq != 0, 1.0, 0.0).astype(_F32)
    has_next = jnp.where((i + 1) % tiles_per_seq != 0, 1.0, 0.0).astype(_F32)
    row = lax.broadcasted_iota(jnp.int32, (tm, TF_FFN), 0)
    first_row, last_row = row == 0, row == tm - 1
    halo_rows = xp_ref.shape[0]

    def conv(u, uh, w):
        prev_row = uh[halo_rows - 1:halo_rows] * has_prev
        next_row = uh[halo_rows:halo_rows + 1] * has_next
        up = jnp.where(first_row, prev_row, pltpu.roll(u, 1, 0))
        un = jnp.where(last_row, next_row, pltpu.roll(u, tm - 1, 0))
        return w[3:4] + up * w[0:1] + u * w[1:2] + un * w[2:3]

    def chunk(c, carry):
        hb, hh = h_scr[...], hh_scr[...]
        wg, wv, cw = wug_ref[c], wuv_ref[c], cw_ref[c]
        gate = conv(_dot(hb, wg), _dot(hh, wg), cw[0:4])
        val = conv(_dot(hb, wv), _dot(hh, wv), cw[4:8])
        act = (jax.nn.gelu(gate, approximate=True) * val).astype(_BF16)
        acc_scr[...] += _dot(act, wd_ref[c])
        return carry

    lax.fori_loop(0, wug_ref.shape[0], chunk, 0)
    y_ref[...] = x + _rms(acc_scr[...], gpost_ref[...])


def _ffn(xf, seq, lw):
    n = xf.shape[0]
    tm = TM_PROJ
    halo = 8
    tok = lambda t: (t, 0)
    resident = lambda a: pl.BlockSpec(a.shape, lambda t: (0,) * a.ndim, pipeline_mode=pl.Buffered(1))
    prev = lambda t: (jnp.maximum(t * (tm // halo) - 1, 0), 0)
    nxt = lambda t: (jnp.minimum((t + 1) * (tm // halo), n // halo - 1), 0)
    return pl.pallas_call(
        functools.partial(_ffn_kernel, tiles_per_seq=seq // tm), grid=(n // tm,),
        in_specs=[pl.BlockSpec((tm, D_MODEL), tok), pl.BlockSpec((halo, D_MODEL), prev),
                  pl.BlockSpec((halo, D_MODEL), nxt), resident(lw['g_ffn_pre']), resident(lw['w_up_g']),
                  resident(lw['w_up_v']), resident(lw['conv']), resident(lw['w_down']), resident(lw['g_ffn_post'])],
        out_specs=pl.BlockSpec((tm, D_MODEL), tok),
        out_shape=jax.ShapeDtypeStruct((n, D_MODEL), _F32),
        scratch_shapes=[pltpu.VMEM((tm, D_MODEL), _BF16), pltpu.VMEM((2 * halo, D_MODEL), _BF16),
                        pltpu.VMEM((tm, D_MODEL), _F32)],
        compiler_params=pltpu.CompilerParams(dimension_semantics=("arbitrary",), vmem_limit_bytes=V7X_VMEM_LIMIT),
        name="ffn",
    )(xf, xf, xf, lw['g_ffn_pre'], lw['w_up_g'], lw['w_up_v'], lw['conv'], lw['w_down'], lw['g_ffn_post'])


def _t5_bucket(rel):
    half = N_BUCKETS // 2
    exact = half // 2
    n = np.abs(rel)
    large = exact + (np.log(np.maximum(n, 1) / exact) / math.log(MAX_DISTANCE / exact) * (half - exact)).astype(np.int32)
    large = np.minimum(large, half - 1)
    return (np.where(rel > 0, half, 0) + np.where(n < exact, n, large)).astype(np.int32)


def _bias_tiles(rel_bias):
    tl = TL_DIL
    rel = np.arange(tl + 2 * BAND)[None, :] - BAND - np.arange(tl)[:, None]
    in_band = jnp.asarray(np.abs(rel) <= BAND)[None]
    tiles = []
    for _, dil in DIL_PATTERNS:
        b = jnp.transpose(rel_bias[_t5_bucket(rel * dil)], (2, 0, 1)).astype(_F32)
        tiles.append(jnp.where(in_band, b, NEG))
    return tiles


def _rope_tables(seq):
    pos = jnp.arange(seq)

    def angles(p, n_freq):
        inv = ROPE_THETA ** (-jnp.arange(n_freq, dtype=_F32) / n_freq)
        return p.astype(_F32)[:, None] * inv[None, :]

    ang_a = angles(pos, ROPE_A // 2)
    ones = jnp.ones((seq, NOPE_A), _F32)
    zpad = jnp.zeros((seq, LANES - NOPE_A - ROPE_A), _F32)
    znope = jnp.zeros((seq, NOPE_A), _F32)
    ca = jnp.concatenate([ones, jnp.cos(ang_a), jnp.cos(ang_a), zpad], axis=-1)
    sa = jnp.concatenate([znope, -jnp.sin(ang_a), jnp.sin(ang_a), zpad], axis=-1)
    ang_r = angles(pos // GRID_W, HD // 4)
    ang_c = angles(pos % GRID_W, HD // 4)
    cc = jnp.concatenate([jnp.cos(ang_r), jnp.cos(ang_r), jnp.cos(ang_c), jnp.cos(ang_c)] * 2, axis=-1)
    sc = jnp.concatenate([-jnp.sin(ang_r), jnp.sin(ang_r), -jnp.sin(ang_c), jnp.sin(ang_c)] * 2, axis=-1)
    return dict(ca=ca, sa=sa, cc=cc, sc=sc)


def _prep_layer_weights(norm_mix_pre, norm_mix_post, w_in, q_lora_norm, w_uq, kv_norm, w_ukv, c_q_norm, c_k_norm,
                        w_out, norm_ffn_pre, norm_ffn_post, w_up, conv_w, conv_b, w_down):
    depth = w_in.shape[0]
    off = np.cumsum((0, Q_LORA, KV_LORA, ROPE_A, H_B * HD, H_B * HD, H_B * HD, H_C * HD, KV_C * HD, KV_C * HD))
    seg = [w_in[..., off[i]:off[i + 1]] for i in range(9)]
    zc = lambda n: jnp.zeros((depth, D_MODEL, n), w_in.dtype)
    kr_seg = jnp.concatenate([zc(NOPE_A), seg[2], zc(LANES - NOPE_A - ROPE_A)], axis=-1)
    order = np.asarray(C_HEAD_ORDER)
    cq = seg[6].reshape(depth, D_MODEL, H_C, HD)[:, :, order].reshape(depth, D_MODEL, H_C * HD)
    w_in_p = jnp.concatenate([seg[0], seg[1], kr_seg, seg[3], seg[4], seg[5], cq, seg[7], seg[8]], axis=-1)
    assert w_in_p.shape[-1] == N_INP

    uq = w_uq.reshape(depth, Q_LORA, H_A, NOPE_A + ROPE_A)
    uq = jnp.pad(uq, ((0, 0), (0, 0), (0, 0), (0, LANES - NOPE_A - ROPE_A))).reshape(depth, Q_LORA, H_A * LANES)
    ukv = w_ukv.reshape(depth, KV_LORA, H_A, NOPE_A + V_A)
    uk = jnp.pad(ukv[..., :NOPE_A], ((0, 0), (0, 0), (0, 0), (0, LANES - NOPE_A))).reshape(depth, KV_LORA, H_A * LANES)
    uv = ukv[..., NOPE_A:].reshape(depth, KV_LORA, H_A * V_A)
    w_ukv_p = jnp.concatenate([uk, uv], axis=-1)

    n_a, n_b = H_A * V_A, H_B * HD
    oc_rows = w_out[:, n_a + n_b:].reshape(depth, H_C, HD, D_MODEL)[:, order].reshape(depth, H_C * HD, D_MODEL)
    w_out_p = jnp.concatenate([w_out[:, :n_a + n_b], oc_rows], axis=1)

    nc = D_FF // TF_FFN
    chunked_cols = lambda a: a.reshape(depth, a.shape[1], nc, TF_FFN).transpose(0, 2, 1, 3)
    conv = jnp.concatenate([conv_w, conv_b[:, None, :]], axis=1)
    conv = jnp.concatenate([chunked_cols(conv[..., :D_FF]), chunked_cols(conv[..., D_FF:])], axis=2)

    row = lambda a: a[:, None, :].astype(_F32)
    tile2 = lambda a: jnp.concatenate([a, a], axis=-1)
    return dict(
        g_mix_pre=row(norm_mix_pre), g_mix_post=row(norm_mix_post), w_in=w_in_p.astype(_BF16),
        g_q=row(q_lora_norm), w_uq=uq.astype(_BF16), g_kv=row(kv_norm), w_ukv=w_ukv_p.astype(_BF16),
        g_cq=row(tile2(c_q_norm)), g_ck=row(tile2(c_k_norm)), w_out=w_out_p.astype(_BF16),
        g_ffn_pre=row(norm_ffn_pre), g_ffn_post=row(norm_ffn_post),
        w_up_g=chunked_cols(w_up[..., :D_FF]).astype(_BF16), w_up_v=chunked_cols(w_up[..., D_FF:]).astype(_BF16),
        conv=conv.astype(_F32), w_down=w_down.reshape(depth, nc, TF_FFN, D_MODEL).astype(_BF16),
    )


def _trunk(x, weights, bias_tiles):
    bsz, seq, _ = x.shape
    assert seq % TM_PROJ == 0 and seq % TQ_ATT == 0 and seq % TK_ATT == 0
    assert all((seq // d) % TL_DIL == 0 for _, d in DIL_PATTERNS)
    n = bsz * seq
    tabs = _rope_tables(seq)
    xf = x.reshape(n, D_MODEL)
    depth = weights['w_in'].shape[0]
    for layer in range(depth):
        lw = {k: v[layer] for k, v in weights.items()}
        qa, ka, va, bq, bk, bv, qc, kc, vc = _inproj(xf, bsz, seq, lw, tabs)
        oa = _flash(qa, ka, va, bsz, seq, H_A // 2, shared_qk=False)
        oc = _flash(qc, kc.reshape(bsz, 1, seq, LANES), vc, bsz, seq, H_C // 2, shared_qk=True)
        parts = [_dilated(bq, bk, bv, bias_tiles[i], bsz, seq, dil) for i, (_, dil) in enumerate(DIL_PATTERNS)]
        xf = _outproj(xf, oa.reshape(n, H_A * V_A), parts, oc.reshape(n, H_C * HD), lw)
        xf = _ffn(xf, seq, lw)
    return xf.reshape(bsz, seq, D_MODEL)


def kernel(x_prompt, x_sample, rel_bias, norm_mix_pre, norm_mix_post, w_in, q_lora_norm, w_uq, kv_norm, w_ukv,
           c_q_norm, c_k_norm, w_out, norm_ffn_pre, norm_ffn_post, w_up, conv_w, conv_b, w_down):
    weights = _prep_layer_weights(norm_mix_pre, norm_mix_post, w_in, q_lora_norm, w_uq, kv_norm, w_ukv, c_q_norm,
                                  c_k_norm, w_out, norm_ffn_pre, norm_ffn_post, w_up, conv_w, conv_b, w_down)
    bias_tiles = _bias_tiles(rel_bias)
    return _trunk(x_prompt, weights, bias_tiles), _trunk(x_sample, weights, bias_tiles)
```
